```python
import math
import jax, jax.numpy as jnp
from jax import lax
import numpy as np

D_MODEL = 2048
BATCH = 4
SEQ = 2048
DEPTH = 1

N_META = 16
EXPAND = 2
D_MIX = EXPAND * D_MODEL
D_CONV_BR = D_MIX // 2
D_SSD = D_MIX - D_CONV_BR
SSD_HEAD_DIM = 64
SSD_HEADS = D_SSD // SSD_HEAD_DIM
SSD_GROUPS = 8
SSD_HEADS_PER_GROUP = SSD_HEADS // SSD_GROUPS
SSD_STATE = 128
SSD_CONV_WIDTH = 4
SSD_CHUNK = 128
D_XBC = D_SSD + 2 * SSD_GROUPS * SSD_STATE
CONF_WIDTH = 31
D_IN_PROJ = 3 * D_CONV_BR + D_SSD + D_XBC + SSD_HEADS
EPS = 1e-5

kernel_name = "hymba_conformer_ssd_hybrid"


def rmsnorm(x, w):
    xf = x.astype(jnp.float32)
    y = xf * lax.rsqrt(jnp.mean(xf * xf, axis=-1, keepdims=True) + EPS)
    return (y * w.astype(jnp.float32)).astype(x.dtype)


def layernorm(x, g, b):
    xf = x.astype(jnp.float32)
    mu = jnp.mean(xf, axis=-1, keepdims=True)
    var = jnp.mean(jnp.square(xf - mu), axis=-1, keepdims=True)
    y = (xf - mu) * lax.rsqrt(var + EPS)
    return (y * g.astype(jnp.float32) + b.astype(jnp.float32)).astype(x.dtype)


def causal_dwconv(u, w, b):
    k = w.shape[0]
    up = jnp.pad(u, ((0, 0), (k - 1, 0), (0, 0)))
    y = lax.conv_general_dilated(
        up, w[:, None, :].astype(u.dtype), window_strides=(1,), padding="VALID",
        dimension_numbers=("NWC", "WIO", "NWC"), feature_group_count=u.shape[-1])
    return y + b.astype(u.dtype)


def ssd_chunked(x, dt, A, Bm, Cm, Dsk):
    out_dtype = x.dtype
    f32 = jnp.float32
    bsz, seqlen = x.shape[0], x.shape[1]
    pad = (-N_META) % SSD_CHUNK
    padw = ((0, 0), (pad, 0), (0, 0), (0, 0))
    x = jnp.pad(x.astype(f32), padw)
    Bm = jnp.pad(Bm.astype(f32), padw)
    Cm = jnp.pad(Cm.astype(f32), padw)
    dt = jnp.pad(dt.astype(f32), ((0, 0), (pad, 0), (0, 0)))
    total = seqlen + pad
    nc = total // SSD_CHUNK
    G, R, P, N, Q = SSD_GROUPS, SSD_HEADS_PER_GROUP, SSD_HEAD_DIM, SSD_STATE, SSD_CHUNK

    x_c = x.reshape(bsz, nc, Q, G, R, P)
    xdt_c = x_c * dt.reshape(bsz, nc, Q, G, R)[..., None]
    dA_c = (dt * A.astype(f32)).reshape(bsz, nc, Q, G, R)
    B_c = Bm.reshape(bsz, nc, Q, G, N)
    C_c = Cm.reshape(bsz, nc, Q, G, N)

    Acum = jnp.cumsum(dA_c, axis=2)
    seg = Acum[:, :, :, None] - Acum[:, :, None]
    causal = jnp.tril(jnp.ones((Q, Q), dtype=bool))[:, :, None, None]
    Lmat = jnp.exp(jnp.where(causal, seg, -jnp.inf))

    CB = jnp.einsum("bcqgn,bcsgn->bcqsg", C_c, B_c)
    y_diag = jnp.einsum("bcqsg,bcqsgr,bcsgrp->bcqgrp", CB, Lmat, xdt_c)

    decay_states = jnp.exp(Acum[:, :, -1:] - Acum)
    states = jnp.einsum("bcsgn,bcsgr,bcsgrp->bcgrpn", B_c, decay_states, xdt_c)
    chunk_decay = jnp.exp(Acum[:, :, -1])

    def step(carry, inp):
        st, dec = inp
        new = carry * dec[..., None, None] + st
        return new, carry

    init = jnp.zeros((bsz, G, R, P, N), f32)
    _, prev = lax.scan(step, init, (jnp.moveaxis(states, 1, 0), jnp.moveaxis(chunk_decay, 1, 0)))
    prev = jnp.moveaxis(prev, 0, 1)

    y_off = jnp.einsum("bcqgn,bcgrpn,bcqgr->bcqgrp", C_c, prev, jnp.exp(Acum))
    y = y_diag + y_off + x_c * Dsk.astype(f32).reshape(G, R)[:, :, None]
    y = y.reshape(bsz, total, SSD_HEADS, P)[:, pad:]
    return y.astype(out_dtype)


def gated_group_rmsnorm(y, z, w):
    v = (y * jax.nn.silu(z)).astype(jnp.float32)
    shp = v.shape
    v = v.reshape(shp[:-1] + (SSD_GROUPS, shp[-1] // SSD_GROUPS))
    v = v * lax.rsqrt(jnp.mean(v * v, axis=-1, keepdims=True) + EPS)
    return (v.reshape(shp) * w.astype(jnp.float32)).astype(y.dtype)


def setup_inputs(seed: int = 0) -> dict:
    key = jax.random.key(seed)
    ks = jax.random.split(key, 16)
    nrm = jax.random.normal
    f32 = jnp.float32
    x = nrm(ks[0], (BATCH, SEQ, D_MODEL), f32)
    meta_tokens = nrm(ks[1], (N_META, D_MODEL), f32)
    norm_w = 1.0 + 0.02 * nrm(ks[2], (DEPTH, D_MODEL), f32)
    w_in = nrm(ks[3], (DEPTH, D_MODEL, D_IN_PROJ), f32) * D_MODEL ** -0.5
    conf_dw_w = nrm(ks[4], (DEPTH, CONF_WIDTH, D_CONV_BR), f32) * CONF_WIDTH ** -0.5
    conf_dw_b = 0.02 * nrm(ks[5], (DEPTH, D_CONV_BR), f32)
    conf_ln_g = 1.0 + 0.02 * nrm(ks[6], (DEPTH, D_CONV_BR), f32)
    conf_ln_b = 0.02 * nrm(ks[7], (DEPTH, D_CONV_BR), f32)
    ssd_conv_w = nrm(ks[8], (DEPTH, SSD_CONV_WIDTH, D_XBC), f32) * SSD_CONV_WIDTH ** -0.5
    ssd_conv_b = 0.02 * nrm(ks[9], (DEPTH, D_XBC), f32)
    u = jax.random.uniform(ks[10], (DEPTH, SSD_HEADS), f32)
    dt0 = jnp.exp(u * (math.log(0.1) - math.log(0.001)) + math.log(0.001))
    dt_bias = dt0 + jnp.log(-jnp.expm1(-dt0))
    A_log = jnp.log(jax.random.uniform(ks[11], (DEPTH, SSD_HEADS), f32, minval=1.0, maxval=16.0))
    D_skip = 1.0 + 0.02 * nrm(ks[12], (DEPTH, SSD_HEADS), f32)
    ssd_norm_w = 1.0 + 0.02 * nrm(ks[13], (DEPTH, D_SSD), f32)
    w_out = nrm(ks[14], (DEPTH, D_MIX, D_MODEL), f32) * D_MIX ** -0.5
    final_norm_w = 1.0 + 0.02 * nrm(ks[15], (D_MODEL,), f32)
    return {"x": x, "meta_tokens": meta_tokens, "norm_w": norm_w, "w_in": w_in,
            "conf_dw_w": conf_dw_w, "conf_dw_b": conf_dw_b, "conf_ln_g": conf_ln_g,
            "conf_ln_b": conf_ln_b, "ssd_conv_w": ssd_conv_w, "ssd_conv_b": ssd_conv_b,
            "dt_bias": dt_bias, "A_log": A_log, "D_skip": D_skip, "ssd_norm_w": ssd_norm_w,
            "w_out": w_out, "final_norm_w": final_norm_w}


def reference(x, meta_tokens, norm_w, w_in, conf_dw_w, conf_dw_b, conf_ln_g, conf_ln_b,
              ssd_conv_w, ssd_conv_b, dt_bias, A_log, D_skip, ssd_norm_w, w_out, final_norm_w):
    bsz = x.shape[0]
    meta = jnp.broadcast_to(meta_tokens[None].astype(x.dtype), (bsz, N_META, D_MODEL))
    h_stream = jnp.concatenate([meta, x], axis=1)
    L = h_stream.shape[1]
    splits = [D_CONV_BR, 2 * D_CONV_BR, 3 * D_CONV_BR, 3 * D_CONV_BR + D_SSD,
              3 * D_CONV_BR + D_SSD + D_XBC]
    for l in range(DEPTH):
        hn = rmsnorm(h_stream, norm_w[l])
        proj = hn @ w_in[l].astype(hn.dtype)
        c_val, c_gate, c_silu, z, xbc, dt_raw = jnp.split(proj, splits, axis=-1)

        u = c_val * jax.nn.sigmoid(c_gate)
        u = causal_dwconv(u, conf_dw_w[l], conf_dw_b[l])
        u = layernorm(u, conf_ln_g[l], conf_ln_b[l])
        y_conv = jax.nn.silu(u) * jax.nn.silu(c_silu)

        xbc = jax.nn.silu(causal_dwconv(xbc, ssd_conv_w[l], ssd_conv_b[l]))
        xs, Bm, Cm = jnp.split(xbc, [D_SSD, D_SSD + SSD_GROUPS * SSD_STATE], axis=-1)
        dt = jax.nn.softplus(dt_raw.astype(jnp.float32) + dt_bias[l].astype(jnp.float32))
        A = -jnp.exp(A_log[l].astype(jnp.float32))
        y_ssd = ssd_chunked(xs.reshape(bsz, L, SSD_HEADS, SSD_HEAD_DIM), dt, A,
                            Bm.reshape(bsz, L, SSD_GROUPS, SSD_STATE),
                            Cm.reshape(bsz, L, SSD_GROUPS, SSD_STATE), D_skip[l])
        y_ssd = gated_group_rmsnorm(y_ssd.reshape(bsz, L, D_SSD), z, ssd_norm_w[l])

        y = jnp.concatenate([y_conv, y_ssd.astype(y_conv.dtype)], axis=-1)
        h_stream = h_stream + (y @ w_out[l].astype(y.dtype)).astype(h_stream.dtype)
    out = rmsnorm(h_stream, final_norm_w)
    return out[:, N_META:]
```

```python
import functools

import jax
import jax.numpy as jnp
from jax import lax
from jax.experimental import pallas as pl
from jax.experimental.pallas import tpu as pltpu

D_MODEL = 2048
N_META = 16
D_CONV = 2048
D_SSD = 2048
HEAD_DIM = 64
N_HEADS = D_SSD // HEAD_DIM
N_GROUPS = 8
GROUP_W = D_SSD // N_GROUPS
D_STATE = 128
SSD_K = 4
CHUNK = 128
D_BC = N_GROUPS * D_STATE
D_XBC = D_SSD + 2 * D_BC
CONF_K = 31
D_MAIN = 3 * D_CONV + D_SSD + D_XBC
EPS = 1e-5

LANES = 128
PAD_ROWS = CHUNK - N_META
HALO = 32
SSD_HALO = 8

ACT_W = D_XBC + 3 * D_CONV
TN = 1024
N_STEPS = ACT_W // TN

VMEM_LIMIT = 56 * 1024 * 1024


def _sigmoid(v):
    return 1.0 / (1.0 + jnp.exp(-v))


def _silu(v):
    return v * _sigmoid(v)


def _split3(v):
    hi = v.astype(jnp.bfloat16)
    r1 = v - hi.astype(jnp.float32)
    mid = r1.astype(jnp.bfloat16)
    lo = (r1 - mid.astype(jnp.float32)).astype(jnp.bfloat16)
    return hi, mid, lo


def _wa_block(j):
    return jnp.where(j < 4, j + 8, jnp.where(j < 6, j - 4, j - 2))


def _wb_block(j):
    return jnp.where(j < 5, 2, 3)


def _inproj_kernel(x_ref, nw_ref, wa_ref, wb_ref, wdt_ref, dtb_ref, act_ref, dt_ref, hn_ref,
                   *, tiles_per_batch):
    i = pl.program_id(0)
    j = pl.program_id(1)

    @pl.when(j == 0)
    def _():
        xf = x_ref[...]
        ms = jnp.mean(xf * xf, axis=-1, keepdims=True)
        hn = xf * lax.rsqrt(ms + EPS) * nw_ref[...]
        hn_ref[...] = hn.astype(jnp.bfloat16)
        raw = jnp.dot(hn_ref[...], wdt_ref[...], preferred_element_type=jnp.float32)
        raw = raw + dtb_ref[...]
        dt = jnp.maximum(raw, 0.0) + jnp.log(1.0 + jnp.exp(-jnp.abs(raw)))
        row = lax.broadcasted_iota(jnp.int32, dt.shape, 0)
        is_pad = jnp.logical_and(i % tiles_per_batch == 0, row < PAD_ROWS)
        dt_ref[...] = jnp.where(is_pad, 0.0, dt)

    hn = hn_ref[...]
    val = jnp.dot(hn, wa_ref[...], preferred_element_type=jnp.float32)

    @pl.when(j < 4)
    def _():
        act_ref[...] = val.astype(act_ref.dtype)

    @pl.when(jnp.logical_and(j >= 4, j < 6))
    def _():
        gate = jnp.dot(hn, wb_ref[...], preferred_element_type=jnp.float32)
        act_ref[...] = (val * _sigmoid(gate)).astype(act_ref.dtype)

    @pl.when(j >= 6)
    def _():
        act_ref[...] = _silu(val).astype(act_ref.dtype)


def _inproj(h_pad, norm_w, w_main, w_dt, dt_bias, *, tm, tiles_per_batch):
    rows = h_pad.shape[0]
    grid = (rows // tm, N_STEPS)
    return pl.pallas_call(
        functools.partial(_inproj_kernel, tiles_per_batch=tiles_per_batch),
        grid=grid,
        in_specs=[
            pl.BlockSpec((tm, D_MODEL), lambda i, j: (i, 0)),
            pl.BlockSpec((1, D_MODEL), lambda i, j: (0, 0)),
            pl.BlockSpec((D_MODEL, TN), lambda i, j: (0, _wa_block(j))),
            pl.BlockSpec((D_MODEL, TN), lambda i, j: (0, _wb_block(j))),
            pl.BlockSpec((D_MODEL, LANES), lambda i, j: (0, 0)),
            pl.BlockSpec((1, LANES), lambda i, j: (0, 0)),
        ],
        out_specs=[
            pl.BlockSpec((tm, TN), lambda i, j: (i, j)),
            pl.BlockSpec((tm, LANES), lambda i, j: (i, 0)),
        ],
        out_shape=[
            jax.ShapeDtypeStruct((rows, ACT_W), jnp.bfloat16),
            jax.ShapeDtypeStruct((rows, LANES), jnp.float32),
        ],
        scratch_shapes=[pltpu.VMEM((tm, D_MODEL), jnp.bfloat16)],
        compiler_params=pltpu.CompilerParams(
            dimension_semantics=("arbitrary", "arbitrary"),
            vmem_limit_bytes=VMEM_LIMIT),
        name="inproj",
    )(h_pad, norm_w, w_main, w_main, w_dt, dt_bias)


N_SLABS = D_CONV // LANES


def _convbr_kernel(u_ref, gs_ref, w_ref, b_ref, g_ref, beta_ref, y_ref, ucat_ref, conv_ref):
    c = pl.program_id(1)

    @pl.when(c == 0)
    def _():
        ucat_ref[:, 0:HALO, :] = jnp.zeros((N_SLABS, HALO, LANES), jnp.float32)

    for s in range(N_SLABS):
        ucat_ref[s, HALO:HALO + CHUNK, :] = u_ref[:, s * LANES:(s + 1) * LANES].astype(jnp.float32)

    off = HALO - (CONF_K - 1)
    for s in range(N_SLABS):
        sl = slice(s * LANES, (s + 1) * LANES)
        acc = jnp.broadcast_to(b_ref[:, sl], (CHUNK, LANES))
        for k in range(CONF_K):
            acc = acc + w_ref[k:k + 1, sl] * ucat_ref[s, off + k:off + k + CHUNK, :]
        conv_ref[:, sl] = acc

    ucat_ref[:, 0:HALO, :] = ucat_ref[:, CHUNK:CHUNK + HALO, :]

    cv = conv_ref[...]
    mu = jnp.mean(cv, axis=-1, keepdims=True)
    d = cv - mu
    var = jnp.mean(d * d, axis=-1, keepdims=True)
    yn = d * lax.rsqrt(var + EPS) * g_ref[...] + beta_ref[...]
    y_ref[...] = (_silu(yn) * gs_ref[...].astype(jnp.float32)).astype(y_ref.dtype)


def _convbr(act, w, b, g, beta, *, bsz, chunks):
    real = chunks - 1
    out_rows = bsz * real * CHUNK
    return pl.pallas_call(
        _convbr_kernel,
        grid=(bsz, chunks),
        in_specs=[
            pl.BlockSpec((CHUNK, D_CONV), lambda b_, c: (b_ * chunks + c, D_XBC // D_CONV)),
            pl.BlockSpec((CHUNK, D_CONV), lambda b_, c: (b_ * chunks + c, D_XBC // D_CONV + 1)),
            pl.BlockSpec((CONF_K, D_CONV), lambda b_, c: (0, 0)),
            pl.BlockSpec((1, D_CONV), lambda b_, c: (0, 0)),
            pl.BlockSpec((1, D_CONV), lambda b_, c: (0, 0)),
            pl.BlockSpec((1, D_CONV), lambda b_, c: (0, 0)),
        ],
        out_specs=pl.BlockSpec((CHUNK, D_CONV), lambda b_, c: (b_ * real + jnp.maximum(c - 1, 0), 0)),
        out_shape=jax.ShapeDtypeStruct((out_rows, D_CONV), jnp.bfloat16),
        scratch_shapes=[
            pltpu.VMEM((N_SLABS, HALO + CHUNK, LANES), jnp.float32),
            pltpu.VMEM((CHUNK, D_CONV), jnp.float32),
        ],
        compiler_params=pltpu.CompilerParams(
            dimension_semantics=("arbitrary", "arbitrary"),
            vmem_limit_bytes=VMEM_LIMIT),
        name="convbr",
    )(act, act, w, b, g, beta)


def _ssd_kernel(xbc_ref, zs_ref, dt_ref, cw_ref, cb_ref, alog_ref, alog32_ref, dfull_ref, nw_ref,
                e3_ref, t3_ref, u_ref, y_ref, xcat_ref, st_ref, xc_ref):
    c = pl.program_id(1)
    f32 = jnp.float32
    bf16 = jnp.bfloat16

    @pl.when(c == 0)
    def _():
        xcat_ref[0:SSD_HALO, :] = jnp.zeros((SSD_HALO, D_XBC), f32)
        st_ref[...] = jnp.zeros(st_ref.shape, f32)

    xcat_ref[SSD_HALO:SSD_HALO + CHUNK, :] = xbc_ref[...].astype(f32)
    off = SSD_HALO - (SSD_K - 1)
    acc = jnp.broadcast_to(cb_ref[...], (CHUNK, D_XBC))
    for k in range(SSD_K):
        acc = acc + cw_ref[k:k + 1, :] * xcat_ref[off + k:off + k + CHUNK, :]
    xc_ref[...] = _silu(acc)
    xcat_ref[0:SSD_HALO, :] = xcat_ref[CHUNK:CHUNK + SSD_HALO, :]

    dt32 = dt_ref[...]
    hi, mid, lo = _split3(dt32)
    dt3 = jnp.concatenate([hi, mid, lo], axis=1)
    dt_full = jnp.dot(dt3, e3_ref[...], preferred_element_type=f32)
    a_full = -jnp.exp(alog_ref[...])
    da_full = dt_full * a_full
    hi, mid, lo = _split3(da_full)
    da3 = jnp.concatenate([hi, mid, lo], axis=0)
    acum = jnp.dot(t3_ref[...], da3, preferred_element_type=f32)
    a_last = acum[CHUNK - 1:CHUNK, :]
    exp_acum = jnp.exp(acum)
    decay_st = jnp.exp(a_last - acum)
    chunk_decay = jnp.exp(a_last)

    a32 = -jnp.exp(alog32_ref[...])
    da32_t = jnp.transpose(dt32 * a32)
    hi, mid, lo = _split3(da32_t)
    dat3 = jnp.concatenate([hi, mid, lo], axis=0)
    acum_t3 = jnp.dot(dat3, u_ref[...], preferred_element_type=f32)
    acum_t = (acum_t3[0:LANES] + acum_t3[LANES:2 * LANES]) + acum_t3[2 * LANES:3 * LANES]

    q_idx = lax.broadcasted_iota(jnp.int32, (CHUNK, CHUNK), 0)
    s_idx = lax.broadcasted_iota(jnp.int32, (CHUNK, CHUNK), 1)
    causal = q_idx >= s_idx
    low_half = s_idx < HEAD_DIM

    for g in range(N_GROUPS):
        gsl = slice(g * GROUP_W, (g + 1) * GROUP_W)
        b_g = xc_ref[:, D_SSD + g * D_STATE:D_SSD + (g + 1) * D_STATE]
        c_g = xc_ref[:, D_SSD + D_BC + g * D_STATE:D_SSD + D_BC + (g + 1) * D_STATE]
        bt_g = jnp.transpose(b_g).astype(bf16)
        c_gb = c_g.astype(bf16)
        cb = jnp.dot(c_gb, bt_g, preferred_element_type=f32)
        cb = jnp.where(causal, cb, 0.0)

        xdt_g = xc_ref[:, gsl] * dt_full[:, gsl]
        st_g = st_ref[:, gsl]
        y_g = jnp.dot(c_gb, st_g.astype(bf16), preferred_element_type=f32) * exp_acum[:, gsl]
        y_g = y_g + xc_ref[:, gsl] * dfull_ref[:, gsl]

        pair_out = []
        for p in range(2):
            psl = slice(g * GROUP_W + p * LANES, g * GROUP_W + (p + 1) * LANES)
            h_even = (g * GROUP_W + p * LANES) // HEAD_DIM
            v = acum[:, psl]
            r = pltpu.roll(v, HEAD_DIM, 1)
            col_e = jnp.where(low_half, v, r)
            col_o = jnp.where(low_half, r, v)
            row_e = acum_t[h_even:h_even + 1, :]
            row_o = acum_t[h_even + 1:h_even + 2, :]
            l_e = jnp.exp(jnp.where(causal, col_e - row_e, -jnp.inf))
            l_o = jnp.exp(jnp.where(causal, col_o - row_o, -jnp.inf))
            xdt_p = xdt_g[:, p * LANES:(p + 1) * LANES]
            x_e = jnp.where(low_half, xdt_p, 0.0).astype(bf16)
            x_o = jnp.where(low_half, 0.0, xdt_p).astype(bf16)
            yd = jnp.dot((cb * l_e).astype(bf16), x_e, preferred_element_type=f32)
            yd = yd + jnp.dot((cb * l_o).astype(bf16), x_o, preferred_element_type=f32)
            pair_out.append(yd)
        y_g = y_g + jnp.concatenate(pair_out, axis=1)

        xs_dec = (xdt_g * decay_st[:, gsl]).astype(bf16)
        st_ref[:, gsl] = st_g * chunk_decay[:, gsl] + jnp.dot(bt_g, xs_dec, preferred_element_type=f32)

        vg = y_g * zs_ref[:, gsl].astype(f32)
        ms = jnp.mean(vg * vg, axis=-1, keepdims=True)
        y_ref[:, gsl] = (vg * lax.rsqrt(ms + EPS) * nw_ref[:, gsl]).astype(y_ref.dtype)


def _ssd(act, dt, cw, cb, alog_full, alog32, d_full, nw, e3, t3, u, *, bsz, chunks):
    real = chunks - 1
    out_rows = bsz * real * CHUNK
    const = lambda b_, c: (0, 0)
    return pl.pallas_call(
        _ssd_kernel,
        grid=(bsz, chunks),
        in_specs=[
            pl.BlockSpec((CHUNK, D_XBC), lambda b_, c: (b_ * chunks + c, 0)),
            pl.BlockSpec((CHUNK, D_SSD), lambda b_, c: (b_ * chunks + c, (D_XBC + 2 * D_CONV) // D_SSD)),
            pl.BlockSpec((CHUNK, LANES), lambda b_, c: (b_ * chunks + c, 0)),
            pl.BlockSpec((SSD_K, D_XBC), const),
            pl.BlockSpec((1, D_XBC), const),
            pl.BlockSpec((1, D_SSD), const),
            pl.BlockSpec((1, LANES), const),
            pl.BlockSpec((1, D_SSD), const),
            pl.BlockSpec((1, D_SSD), const),
            pl.BlockSpec((3 * LANES, D_SSD), const),
            pl.BlockSpec((CHUNK, 3 * CHUNK), const),
            pl.BlockSpec((CHUNK, CHUNK), const),
        ],
        out_specs=pl.BlockSpec((CHUNK, D_SSD), lambda b_, c: (b_ * real + jnp.maximum(c - 1, 0), 0)),
        out_shape=jax.ShapeDtypeStruct((out_rows, D_SSD), jnp.bfloat16),
        scratch_shapes=[
            pltpu.VMEM((SSD_HALO + CHUNK, D_XBC), jnp.float32),
            pltpu.VMEM((D_STATE, D_SSD), jnp.float32),
            pltpu.VMEM((CHUNK, D_XBC), jnp.float32),
        ],
        compiler_params=pltpu.CompilerParams(
            dimension_semantics=("arbitrary", "arbitrary"),
            vmem_limit_bytes=VMEM_LIMIT),
        name="ssd",
    )(act, act, dt, cw, cb, alog_full, alog32, d_full, nw, e3, t3, u)


def _outproj_kernel(x_ref, yc_ref, ys_ref, wc_ref, ws_ref, fw_ref, o_ref):
    acc = jnp.dot(yc_ref[...], wc_ref[...], preferred_element_type=jnp.float32)
    acc = acc + jnp.dot(ys_ref[...], ws_ref[...], preferred_element_type=jnp.float32)
    h = x_ref[...] + acc
    ms = jnp.mean(h * h, axis=-1, keepdims=True)
    o_ref[...] = h * lax.rsqrt(ms + EPS) * fw_ref[...]


def _outproj(x2d, y_conv, y_ssd, w_c, w_s, final_w, *, tm):
    rows = x2d.shape[0]
    const = lambda i: (0, 0)
    return pl.pallas_call(
        _outproj_kernel,
        grid=(rows // tm,),
        in_specs=[
            pl.BlockSpec((tm, D_MODEL), lambda i: (i, 0)),
            pl.BlockSpec((tm, D_CONV), lambda i: (i, 0)),
            pl.BlockSpec((tm, D_SSD), lambda i: (i, 0)),
            pl.BlockSpec((D_CONV, D_MODEL), const),
            pl.BlockSpec((D_SSD, D_MODEL), const),
            pl.BlockSpec((1, D_MODEL), const),
        ],
        out_specs=pl.BlockSpec((tm, D_MODEL), lambda i: (i, 0)),
        out_shape=jax.ShapeDtypeStruct((rows, D_MODEL), jnp.float32),
        compiler_params=pltpu.CompilerParams(
            dimension_semantics=("arbitrary",),
            vmem_limit_bytes=VMEM_LIMIT),
        name="outproj",
    )(x2d, y_conv, y_ssd, w_c, w_s, final_w)


def kernel(x, meta_tokens, norm_w, w_in, conf_dw_w, conf_dw_b, conf_ln_g, conf_ln_b,
           ssd_conv_w, ssd_conv_b, dt_bias, A_log, D_skip, ssd_norm_w, w_out, final_norm_w):
    bsz, seq, _ = x.shape
    f32 = jnp.float32
    bf16 = jnp.bfloat16
    chunks = 1 + seq // CHUNK
    t_pad = chunks * CHUNK

    meta = jnp.broadcast_to(meta_tokens[None].astype(x.dtype), (bsz, N_META, D_MODEL))
    h_pad = jnp.concatenate([jnp.zeros((bsz, PAD_ROWS, D_MODEL), x.dtype), meta, x], axis=1)
    h_pad = h_pad.reshape(bsz * t_pad, D_MODEL)

    w_main = w_in[0, :, :D_MAIN].astype(bf16)
    w_dt = jnp.pad(w_in[0, :, D_MAIN:], ((0, 0), (0, LANES - N_HEADS))).astype(bf16)
    dtb = jnp.pad(dt_bias[0].astype(f32), (0, LANES - N_HEADS))[None]

    tiles_per_batch = 2
    act, dt = _inproj(h_pad, norm_w[0][None].astype(f32), w_main, w_dt, dtb,
                      tm=t_pad // tiles_per_batch, tiles_per_batch=tiles_per_batch)

    y_conv = _convbr(act, conf_dw_w[0].astype(f32), conf_dw_b[0][None].astype(f32),
                     conf_ln_g[0][None].astype(f32), conf_ln_b[0][None].astype(f32),
                     bsz=bsz, chunks=chunks)

    head_of_lane = jnp.arange(D_SSD) // HEAD_DIM
    e1 = (jnp.arange(LANES)[:, None] == head_of_lane[None, :]).astype(bf16)
    e3 = jnp.concatenate([e1, e1, e1], axis=0)
    tri = (jnp.arange(CHUNK)[:, None] >= jnp.arange(CHUNK)[None, :]).astype(bf16)
    t3 = jnp.concatenate([tri, tri, tri], axis=1)
    u = tri.T

    alog_full = jnp.repeat(A_log[0].astype(f32), HEAD_DIM)[None]
    alog32 = jnp.pad(A_log[0].astype(f32), (0, LANES - N_HEADS))[None]
    d_full = jnp.repeat(D_skip[0].astype(f32), HEAD_DIM)[None]

    y_ssd = _ssd(act, dt, ssd_conv_w[0].astype(f32), ssd_conv_b[0][None].astype(f32),
                 alog_full, alog32, d_full, ssd_norm_w[0][None].astype(f32), e3, t3, u,
                 bsz=bsz, chunks=chunks)

    w_c = w_out[0, :D_CONV].astype(bf16)
    w_s = w_out[0, D_CONV:].astype(bf16)
    out = _outproj(x.reshape(bsz * seq, D_MODEL), y_conv, y_ssd, w_c, w_s,
                   final_norm_w[None].astype(f32), tm=512)
    return out.reshape(bsz, seq, D_MODEL)
```

```python
import functools

import jax
import jax.numpy as jnp
from jax import lax
from jax.experimental import pallas as pl
from jax.experimental.pallas import tpu as pltpu

D_MODEL = 2048
N_META = 16
D_CONV = 2048
D_SSD = 2048
HEAD_DIM = 64
N_HEADS = D_SSD // HEAD_DIM
N_GROUPS = 8
GROUP_W = D_SSD // N_GROUPS
D_STATE = 128
SSD_K = 4
CHUNK = 128
D_BC = N_GROUPS * D_STATE
D_XBC = D_SSD + 2 * D_BC
CONF_K = 31
D_MAIN = 3 * D_CONV + D_SSD + D_XBC
EPS = 1e-5

LANES = 128
PAD_ROWS = CHUNK - N_META
HALO = 32
SSD_HALO = 8

ACT_W = D_XBC + 3 * D_CONV
TN = 512
N_STEPS = ACT_W // TN

VMEM_LIMIT = 56 * 1024 * 1024


def _sigmoid(v):
    return 1.0 / (1.0 + jnp.exp(-v))


def _silu(v):
    return v * _sigmoid(v)


def _split3(v):
    hi = v.astype(jnp.bfloat16)
    r1 = v - hi.astype(jnp.float32)
    mid = r1.astype(jnp.bfloat16)
    lo = (r1 - mid.astype(jnp.float32)).astype(jnp.bfloat16)
    return hi, mid, lo


XBC_STEPS = D_XBC // TN
GLU_STEPS = D_CONV // TN
W_XBC0 = (3 * D_CONV + D_SSD) // TN
W_GATE0 = D_CONV // TN


def _wa_block(j):
    return jnp.where(j < XBC_STEPS, j + W_XBC0,
                     jnp.where(j < XBC_STEPS + GLU_STEPS, j - XBC_STEPS, j - XBC_STEPS + GLU_STEPS))


def _wb_block(j):
    return jnp.clip(j - XBC_STEPS + W_GATE0, W_GATE0, W_GATE0 + GLU_STEPS - 1)


def _dot_wt(lhs, wt_ref):
    return lax.dot_general(lhs, wt_ref[...].astype(jnp.bfloat16), (((1,), (1,)), ((), ())),
                           preferred_element_type=jnp.float32)


def _inproj_kernel(x_ref, nw_ref, wa_ref, wb_ref, wdt_ref, dtb_ref, act_ref, dt_ref, hn_ref,
                   *, tiles_per_batch):
    i = pl.program_id(0)
    j = pl.program_id(1)

    @pl.when(j == 0)
    def _():
        xf = x_ref[...]
        ms = jnp.mean(xf * xf, axis=-1, keepdims=True)
        hn = xf * lax.rsqrt(ms + EPS) * nw_ref[...]
        hn_ref[...] = hn.astype(jnp.bfloat16)
        raw = _dot_wt(hn_ref[...], wdt_ref)
        lane = lax.broadcasted_iota(jnp.int32, raw.shape, 1)
        raw = jnp.where(lane < N_HEADS, raw, 0.0) + dtb_ref[...]
        dt = jnp.maximum(raw, 0.0) + jnp.log(1.0 + jnp.exp(-jnp.abs(raw)))
        row = lax.broadcasted_iota(jnp.int32, dt.shape, 0)
        is_pad = jnp.logical_and(i % tiles_per_batch == 0, row < PAD_ROWS)
        dt_ref[...] = jnp.where(is_pad, 0.0, dt)

    hn = hn_ref[...]
    val = _dot_wt(hn, wa_ref)

    @pl.when(j < XBC_STEPS)
    def _():
        act_ref[...] = val.astype(act_ref.dtype)

    @pl.when(jnp.logical_and(j >= XBC_STEPS, j < XBC_STEPS + GLU_STEPS))
    def _():
        gate = _dot_wt(hn, wb_ref)
        act_ref[...] = (val * _sigmoid(gate)).astype(act_ref.dtype)

    @pl.when(j >= XBC_STEPS + GLU_STEPS)
    def _():
        act_ref[...] = _silu(val).astype(act_ref.dtype)


def _inproj(h_pad, norm_w, w_t, dt_bias, *, tm, tiles_per_batch):
    rows = h_pad.shape[0]
    grid = (rows // tm, N_STEPS)
    return pl.pallas_call(
        functools.partial(_inproj_kernel, tiles_per_batch=tiles_per_batch),
        grid=grid,
        in_specs=[
            pl.BlockSpec((tm, D_MODEL), lambda i, j: (i, 0)),
            pl.BlockSpec((1, D_MODEL), lambda i, j: (0, 0)),
            pl.BlockSpec((TN, D_MODEL), lambda i, j: (_wa_block(j), 0)),
            pl.BlockSpec((TN, D_MODEL), lambda i, j: (_wb_block(j), 0)),
            pl.BlockSpec((LANES, D_MODEL), lambda i, j: (D_MAIN // LANES, 0)),
            pl.BlockSpec((1, LANES), lambda i, j: (0, 0)),
        ],
        out_specs=[
            pl.BlockSpec((tm, TN), lambda i, j: (i, j)),
            pl.BlockSpec((tm, LANES), lambda i, j: (i, 0)),
        ],
        out_shape=[
            jax.ShapeDtypeStruct((rows, ACT_W), jnp.bfloat16),
            jax.ShapeDtypeStruct((rows, LANES), jnp.float32),
        ],
        scratch_shapes=[pltpu.VMEM((tm, D_MODEL), jnp.bfloat16)],
        compiler_params=pltpu.CompilerParams(
            dimension_semantics=("arbitrary", "arbitrary"),
            vmem_limit_bytes=VMEM_LIMIT),
        name="inproj",
    )(h_pad, norm_w, w_t, w_t, w_t, dt_bias)


N_SLABS = D_CONV // LANES


def _convbr_kernel(u_ref, gs_ref, w_ref, b_ref, g_ref, beta_ref, y_ref, ucat_ref, conv_ref):
    c = pl.program_id(1)

    @pl.when(c == 0)
    def _():
        ucat_ref[:, 0:HALO, :] = jnp.zeros((N_SLABS, HALO, LANES), jnp.float32)

    for s in range(N_SLABS):
        ucat_ref[s, HALO:HALO + CHUNK, :] = u_ref[:, s * LANES:(s + 1) * LANES].astype(jnp.float32)

    off = HALO - (CONF_K - 1)
    for s in range(N_SLABS):
        sl = slice(s * LANES, (s + 1) * LANES)
        acc = jnp.broadcast_to(b_ref[:, sl], (CHUNK, LANES))
        for k in range(CONF_K):
            acc = acc + w_ref[k:k + 1, sl] * ucat_ref[s, off + k:off + k + CHUNK, :]
        conv_ref[:, sl] = acc

    ucat_ref[:, 0:HALO, :] = ucat_ref[:, CHUNK:CHUNK + HALO, :]

    cv = conv_ref[...]
    mu = jnp.mean(cv, axis=-1, keepdims=True)
    d = cv - mu
    var = jnp.mean(d * d, axis=-1, keepdims=True)
    yn = d * lax.rsqrt(var + EPS) * g_ref[...] + beta_ref[...]
    y_ref[...] = (_silu(yn) * gs_ref[...].astype(jnp.float32)).astype(y_ref.dtype)


def _convbr(act, w, b, g, beta, *, bsz, chunks):
    real = chunks - 1
    out_rows = bsz * real * CHUNK
    return pl.pallas_call(
        _convbr_kernel,
        grid=(bsz, chunks),
        in_specs=[
            pl.BlockSpec((CHUNK, D_CONV), lambda b_, c: (b_ * chunks + c, D_XBC // D_CONV)),
            pl.BlockSpec((CHUNK, D_CONV), lambda b_, c: (b_ * chunks + c, D_XBC // D_CONV + 1)),
            pl.BlockSpec((CONF_K, D_CONV), lambda b_, c: (0, 0)),
            pl.BlockSpec((1, D_CONV), lambda b_, c: (0, 0)),
            pl.BlockSpec((1, D_CONV), lambda b_, c: (0, 0)),
            pl.BlockSpec((1, D_CONV), lambda b_, c: (0, 0)),
        ],
        out_specs=pl.BlockSpec((CHUNK, D_CONV), lambda b_, c: (b_ * real + jnp.maximum(c - 1, 0), 0)),
        out_shape=jax.ShapeDtypeStruct((out_rows, D_CONV), jnp.bfloat16),
        scratch_shapes=[
            pltpu.VMEM((N_SLABS, HALO + CHUNK, LANES), jnp.float32),
            pltpu.VMEM((CHUNK, D_CONV), jnp.float32),
        ],
        compiler_params=pltpu.CompilerParams(
            dimension_semantics=("arbitrary", "arbitrary"),
            vmem_limit_bytes=VMEM_LIMIT),
        name="convbr",
    )(act, act, w, b, g, beta)


def _ssd_kernel(xbc_ref, zs_ref, dt_ref, cw_ref, cb_ref, alog_ref, alog32_ref, dfull_ref, nw_ref,
                e3_ref, t3_ref, u_ref, y_ref, xcat_ref, st_ref, xc_ref):
    c = pl.program_id(1)
    f32 = jnp.float32
    bf16 = jnp.bfloat16

    @pl.when(c == 0)
    def _():
        xcat_ref[:, 0:SSD_HALO, :] = jnp.zeros((D_XBC // LANES, SSD_HALO, LANES), f32)
        st_ref[...] = jnp.zeros(st_ref.shape, f32)

    off = SSD_HALO - (SSD_K - 1)
    for s in range(D_XBC // LANES):
        sl = slice(s * LANES, (s + 1) * LANES)
        xcat_ref[s, SSD_HALO:SSD_HALO + CHUNK, :] = xbc_ref[:, sl].astype(f32)
        acc = jnp.broadcast_to(cb_ref[:, sl], (CHUNK, LANES))
        for k in range(SSD_K):
            acc = acc + cw_ref[k:k + 1, sl] * xcat_ref[s, off + k:off + k + CHUNK, :]
        xc_ref[:, sl] = _silu(acc)
    xcat_ref[:, 0:SSD_HALO, :] = xcat_ref[:, CHUNK:CHUNK + SSD_HALO, :]

    dt32 = dt_ref[...]
    hi, mid, lo = _split3(dt32)
    dt3 = jnp.concatenate([hi, mid, lo], axis=1)
    dt_full = jnp.dot(dt3, e3_ref[...], preferred_element_type=f32)
    a_full = -jnp.exp(alog_ref[...])
    da_full = dt_full * a_full
    hi, mid, lo = _split3(da_full)
    da3 = jnp.concatenate([hi, mid, lo], axis=0)
    acum = jnp.dot(t3_ref[...], da3, preferred_element_type=f32)
    a_last = acum[CHUNK - 1:CHUNK, :]
    exp_acum = jnp.exp(acum)
    decay_st = jnp.exp(a_last - acum)
    chunk_decay = jnp.exp(a_last)

    a32 = -jnp.exp(alog32_ref[...])
    da32_t = jnp.transpose(dt32 * a32)
    hi, mid, lo = _split3(da32_t)
    dat3 = jnp.concatenate([hi, mid, lo], axis=0)
    acum_t3 = jnp.dot(dat3, u_ref[...], preferred_element_type=f32)
    acum_t = (acum_t3[0:LANES] + acum_t3[LANES:2 * LANES]) + acum_t3[2 * LANES:3 * LANES]

    q_idx = lax.broadcasted_iota(jnp.int32, (CHUNK, CHUNK), 0)
    s_idx = lax.broadcasted_iota(jnp.int32, (CHUNK, CHUNK), 1)
    causal = q_idx >= s_idx
    low_half = s_idx < HEAD_DIM

    for g in range(N_GROUPS):
        gsl = slice(g * GROUP_W, (g + 1) * GROUP_W)
        b_g = xc_ref[:, D_SSD + g * D_STATE:D_SSD + (g + 1) * D_STATE]
        c_g = xc_ref[:, D_SSD + D_BC + g * D_STATE:D_SSD + D_BC + (g + 1) * D_STATE]
        bt_g = jnp.transpose(b_g).astype(bf16)
        c_gb = c_g.astype(bf16)
        cb = jnp.dot(c_gb, bt_g, preferred_element_type=f32)
        cb = jnp.where(causal, cb, 0.0)

        xdt_g = xc_ref[:, gsl] * dt_full[:, gsl]
        st_g = st_ref[:, gsl]
        y_g = jnp.dot(c_gb, st_g.astype(bf16), preferred_element_type=f32) * exp_acum[:, gsl]
        y_g = y_g + xc_ref[:, gsl] * dfull_ref[:, gsl]

        pair_out = []
        for p in range(2):
            psl = slice(g * GROUP_W + p * LANES, g * GROUP_W + (p + 1) * LANES)
            h_even = (g * GROUP_W + p * LANES) // HEAD_DIM
            v = acum[:, psl]
            r = pltpu.roll(v, HEAD_DIM, 1)
            col_e = jnp.where(low_half, v, r)
            col_o = jnp.where(low_half, r, v)
            row_e = acum_t[h_even:h_even + 1, :]
            row_o = acum_t[h_even + 1:h_even + 2, :]
            l_e = jnp.exp(jnp.where(causal, col_e - row_e, -jnp.inf))
            l_o = jnp.exp(jnp.where(causal, col_o - row_o, -jnp.inf))
            xdt_p = xdt_g[:, p * LANES:(p + 1) * LANES]
            x_e = jnp.where(low_half, xdt_p, 0.0).astype(bf16)
            x_o = jnp.where(low_half, 0.0, xdt_p).astype(bf16)
            yd = jnp.dot((cb * l_e).astype(bf16), x_e, preferred_element_type=f32)
            yd = yd + jnp.dot((cb * l_o).astype(bf16), x_o, preferred_element_type=f32)
            pair_out.append(yd)
        y_g = y_g + jnp.concatenate(pair_out, axis=1)

        xs_dec = (xdt_g * decay_st[:, gsl]).astype(bf16)
        st_ref[:, gsl] = st_g * chunk_decay[:, gsl] + jnp.dot(bt_g, xs_dec, preferred_element_type=f32)

        vg = y_g * zs_ref[:, gsl].astype(f32)
        ms = jnp.mean(vg * vg, axis=-1, keepdims=True)
        y_ref[:, gsl] = (vg * lax.rsqrt(ms + EPS) * nw_ref[:, gsl]).astype(y_ref.dtype)


def _ssd(act, dt, cw, cb, alog_full, alog32, d_full, nw, e3, t3, u, *, bsz, chunks):
    real = chunks - 1
    out_rows = bsz * real * CHUNK
    const = lambda b_, c: (0, 0)
    return pl.pallas_call(
        _ssd_kernel,
        grid=(bsz, chunks),
        in_specs=[
            pl.BlockSpec((CHUNK, D_XBC), lambda b_, c: (b_ * chunks + c, 0)),
            pl.BlockSpec((CHUNK, D_SSD), lambda b_, c: (b_ * chunks + c, (D_XBC + 2 * D_CONV) // D_SSD)),
            pl.BlockSpec((CHUNK, LANES), lambda b_, c: (b_ * chunks + c, 0)),
            pl.BlockSpec((SSD_K, D_XBC), const),
            pl.BlockSpec((1, D_XBC), const),
            pl.BlockSpec((1, D_SSD), const),
            pl.BlockSpec((1, LANES), const),
            pl.BlockSpec((1, D_SSD), const),
            pl.BlockSpec((1, D_SSD), const),
            pl.BlockSpec((3 * LANES, D_SSD), const),
            pl.BlockSpec((CHUNK, 3 * CHUNK), const),
            pl.BlockSpec((CHUNK, CHUNK), const),
        ],
        out_specs=pl.BlockSpec((CHUNK, D_SSD), lambda b_, c: (b_ * real + jnp.maximum(c - 1, 0), 0)),
        out_shape=jax.ShapeDtypeStruct((out_rows, D_SSD), jnp.bfloat16),
        scratch_shapes=[
            pltpu.VMEM((D_XBC // LANES, SSD_HALO + CHUNK, LANES), jnp.float32),
            pltpu.VMEM((D_STATE, D_SSD), jnp.float32),
            pltpu.VMEM((CHUNK, D_XBC), jnp.float32),
        ],
        compiler_params=pltpu.CompilerParams(
            dimension_semantics=("arbitrary", "arbitrary"),
            vmem_limit_bytes=VMEM_LIMIT),
        name="ssd",
    )(act, act, dt, cw, cb, alog_full, alog32, d_full, nw, e3, t3, u)


def _outproj_kernel(x_ref, yc_ref, ys_ref, wc_ref, ws_ref, fw_ref, o_ref):
    acc = jnp.dot(yc_ref[...], wc_ref[...], preferred_element_type=jnp.float32)
    acc = acc + jnp.dot(ys_ref[...], ws_ref[...], preferred_element_type=jnp.float32)
    h = x_ref[...] + acc
    ms = jnp.mean(h * h, axis=-1, keepdims=True)
    o_ref[...] = h * lax.rsqrt(ms + EPS) * fw_ref[...]


def _outproj(x2d, y_conv, y_ssd, w_c, w_s, final_w, *, tm):
    rows = x2d.shape[0]
    const = lambda i: (0, 0)
    return pl.pallas_call(
        _outproj_kernel,
        grid=(rows // tm,),
        in_specs=[
            pl.BlockSpec((tm, D_MODEL), lambda i: (i, 0)),
            pl.BlockSpec((tm, D_CONV), lambda i: (i, 0)),
            pl.BlockSpec((tm, D_SSD), lambda i: (i, 0)),
            pl.BlockSpec((D_CONV, D_MODEL), const),
            pl.BlockSpec((D_SSD, D_MODEL), const),
            pl.BlockSpec((1, D_MODEL), const),
        ],
        out_specs=pl.BlockSpec((tm, D_MODEL), lambda i: (i, 0)),
        out_shape=jax.ShapeDtypeStruct((rows, D_MODEL), jnp.float32),
        compiler_params=pltpu.CompilerParams(
            dimension_semantics=("arbitrary",),
            vmem_limit_bytes=VMEM_LIMIT),
        name="outproj",
    )(x2d, y_conv, y_ssd, w_c, w_s, final_w)


def kernel(x, meta_tokens, norm_w, w_in, conf_dw_w, conf_dw_b, conf_ln_g, conf_ln_b,
           ssd_conv_w, ssd_conv_b, dt_bias, A_log, D_skip, ssd_norm_w, w_out, final_norm_w):
    bsz, seq, _ = x.shape
    f32 = jnp.float32
    bf16 = jnp.bfloat16
    chunks = 1 + seq // CHUNK
    t_pad = chunks * CHUNK

    meta = jnp.broadcast_to(meta_tokens[None].astype(x.dtype), (bsz, N_META, D_MODEL))
    h_pad = jnp.concatenate([jnp.zeros((bsz, PAD_ROWS, D_MODEL), x.dtype), meta, x], axis=1)
    h_pad = h_pad.reshape(bsz * t_pad, D_MODEL)

    dtb = jnp.pad(dt_bias[0].astype(f32), (0, LANES - N_HEADS))[None]

    tiles_per_batch = 2
    act, dt = _inproj(h_pad, norm_w[0][None].astype(f32), jnp.transpose(w_in[0]).astype(f32), dtb,
                      tm=t_pad // tiles_per_batch, tiles_per_batch=tiles_per_batch)

    y_conv = _convbr(act, conf_dw_w[0].astype(f32), conf_dw_b[0][None].astype(f32),
                     conf_ln_g[0][None].astype(f32), conf_ln_b[0][None].astype(f32),
                     bsz=bsz, chunks=chunks)

    head_of_lane = jnp.arange(D_SSD) // HEAD_DIM
    e1 = (jnp.arange(LANES)[:, None] == head_of_lane[None, :]).astype(bf16)
    e3 = jnp.concatenate([e1, e1, e1], axis=0)
    tri = (jnp.arange(CHUNK)[:, None] >= jnp.arange(CHUNK)[None, :]).astype(bf16)
    t3 = jnp.concatenate([tri, tri, tri], axis=1)
    u = tri.T

    alog_full = jnp.repeat(A_log[0].astype(f32), HEAD_DIM)[None]
    alog32 = jnp.pad(A_log[0].astype(f32), (0, LANES - N_HEADS))[None]
    d_full = jnp.repeat(D_skip[0].astype(f32), HEAD_DIM)[None]

    y_ssd = _ssd(act, dt, ssd_conv_w[0].astype(f32), ssd_conv_b[0][None].astype(f32),
                 alog_full, alog32, d_full, ssd_norm_w[0][None].astype(f32), e3, t3, u,
                 bsz=bsz, chunks=chunks)

    w_c = w_out[0, :D_CONV].astype(bf16)
    w_s = w_out[0, D_CONV:].astype(bf16)
    out = _outproj(x.reshape(bsz * seq, D_MODEL), y_conv, y_ssd, w_c, w_s,
                   final_norm_w[None].astype(f32), tm=512)
    return out.reshape(bsz, seq, D_MODEL)
```
